```python
import jax, jax.numpy as jnp
from jax import lax
import numpy as np

D_MODEL = 1024
BATCH = 8
SEQ = 2048
DEPTH = 2

GRID_W = 64
CTX_LEN = 256
N_MIXERS = 2
N_POOL_LAYERS = (DEPTH + 1) // 2
N_DELTA_LAYERS = DEPTH // 2
NORM_EPS = 1e-6
POOL_WINDOWS = (2, 4, 8, 16)
N_POOL_GROUPS = 4
POOL_GROUP = D_MODEL // N_POOL_GROUPS
DN_HEADS = 8
DN_HEAD_DIM = D_MODEL // DN_HEADS
DN_CONV = 5
DN_CHUNK = 64
PEER_HEADS = 8
PEER_KEY_DIM = 256
PEER_N_KEYS = 128
PEER_N_EXPERTS = PEER_N_KEYS ** 2
PEER_TOPK = 16
PEER_BLOCK = 128

kernel_name = "hybrid_pool_deltanet_peer_dit"


def rmsnorm(h, gain):
    hf = h.astype(jnp.float32)
    y = hf * lax.rsqrt(jnp.mean(hf * hf, axis=-1, keepdims=True) + NORM_EPS)
    return (y * gain.astype(jnp.float32)).astype(h.dtype)


def modulate(h, gain, shift, scale):
    return rmsnorm(h, gain) * (1.0 + scale) + shift


def box_mean(u, axis, w):
    n = u.shape[axis]
    cs = jnp.cumsum(u, axis=axis)
    cs = jnp.concatenate([jnp.zeros_like(lax.slice_in_dim(cs, 0, 1, axis=axis)), cs], axis=axis)
    t = jnp.arange(n)
    lo = jnp.clip(t - w // 2, 0, n)
    hi = jnp.clip(t - w // 2 + w, 0, n)
    s = jnp.take(cs, hi, axis=axis) - jnp.take(cs, lo, axis=axis)
    shape = [1] * u.ndim
    shape[axis] = n
    return s / (hi - lo).astype(u.dtype).reshape(shape)


def pool_mixer(h, w_groups, scale, on_grid):
    B, L, D = h.shape
    hf = h.astype(jnp.float32).reshape(B, L, N_POOL_GROUPS, POOL_GROUP)
    if on_grid:
        rows = L // GRID_W
        hf = hf.reshape(B, rows, GRID_W, N_POOL_GROUPS, POOL_GROUP)
    groups = []
    for gi, w in enumerate(POOL_WINDOWS):
        xg = hf[..., gi, :]
        pooled = box_mean(box_mean(xg, 1, w), 2, w) if on_grid else box_mean(xg, 1, w)
        groups.append(pooled - xg)
    p = jnp.stack(groups, axis=-2).reshape(B, L, N_POOL_GROUPS, POOL_GROUP).astype(h.dtype)
    y = jnp.einsum('blgi,gio->blgo', p, w_groups).reshape(B, L, D)
    return y * scale


def short_conv(u, w):
    ch = u.shape[-1]
    y = lax.conv_general_dilated(u, w[:, None, :].astype(u.dtype), window_strides=(1,),
                                 padding=[(DN_CONV // 2, DN_CONV // 2)],
                                 dimension_numbers=('NWC', 'WIO', 'NWC'), feature_group_count=ch)
    return jax.nn.silu(y)


def l2norm(u):
    return u * lax.rsqrt(jnp.sum(u * u, axis=-1, keepdims=True) + NORM_EPS)


def dn_project(h, w_in, conv_w, a_log, dt_bias):
    B, L, D = h.shape
    p = h @ w_in
    qkv = short_conv(p[..., :3 * D], conv_w).astype(jnp.float32)
    qkv = qkv.reshape(B, L, 3, DN_HEADS, DN_HEAD_DIM).transpose(2, 0, 3, 1, 4)
    q = l2norm(qkv[0]) * (DN_HEAD_DIM ** -0.5)
    k = l2norm(qkv[1])
    v = qkv[2]
    z = p[..., 3 * D:4 * D]
    ab = p[..., 4 * D:].astype(jnp.float32).reshape(B, L, 2, 2, DN_HEADS).transpose(2, 3, 0, 4, 1)
    g = -jnp.exp(a_log.astype(jnp.float32))[:, None, :, None] * jax.nn.softplus(
        ab[:, 0] + dt_bias.astype(jnp.float32)[:, None, :, None])
    beta = jax.nn.sigmoid(ab[:, 1])
    return q, k, v, z, g, beta


def gated_delta_chunk(q, k, v, beta, g, s0, with_output):
    B, H, L, dk = q.shape
    dv = v.shape[-1]
    n = L // DN_CHUNK
    q = q.reshape(B, H, n, DN_CHUNK, dk)
    k = k.reshape(B, H, n, DN_CHUNK, dk)
    v = v.reshape(B, H, n, DN_CHUNK, dv)
    beta = beta.reshape(B, H, n, DN_CHUNK)
    g = jnp.cumsum(g.reshape(B, H, n, DN_CHUNK), axis=-1)
    kb = k * beta[..., None]
    vb = v * beta[..., None]
    idx = jnp.arange(DN_CHUNK)
    incl = idx[:, None] >= idx[None, :]
    strict = idx[:, None] > idx[None, :]
    decay = jnp.exp(jnp.where(incl, g[..., :, None] - g[..., None, :], -jnp.inf))
    a = jnp.where(strict, jnp.einsum('bhnid,bhnjd->bhnij', kb, k) * decay, 0.0)
    eye = jnp.eye(DN_CHUNK, dtype=q.dtype)
    t_inv = lax.linalg.triangular_solve(eye + a, jnp.broadcast_to(eye, a.shape), left_side=True,
                                        lower=True, unit_diagonal=True)
    u = t_inv @ vb
    w = t_inv @ (kb * jnp.exp(g)[..., None])
    kg = k * jnp.exp(g[..., -1:] - g)[..., None]
    glast = jnp.exp(g[..., -1])
    seq_first = lambda t: jnp.moveaxis(t, 2, 0)
    if with_output:
        attn = jnp.einsum('bhnid,bhnjd->bhnij', q, k) * decay
        qg = q * jnp.exp(g)[..., None]
        xs = tuple(seq_first(t) for t in (kg, u, w, glast, qg, attn))
    else:
        xs = tuple(seq_first(t) for t in (kg, u, w, glast))

    def step(s, inp):
        kg_i, u_i, w_i, gl_i = inp[:4]
        v_new = u_i - w_i @ s
        s_new = s * gl_i[..., None, None] + jnp.swapaxes(kg_i, -1, -2) @ v_new
        if with_output:
            qg_i, attn_i = inp[4:]
            return s_new, qg_i @ s + attn_i @ v_new
        return s_new, None

    s_fin, o = lax.scan(step, s0, xs)
    if with_output:
        o = jnp.moveaxis(o, 0, 2).reshape(B, H, L, dv)
    return o, s_fin


def dn_output(o, z, norm_g, w_out):
    B, H, L, dh = o.shape
    o = o.transpose(0, 2, 1, 3)
    o = o * lax.rsqrt(jnp.mean(o * o, axis=-1, keepdims=True) + NORM_EPS) * norm_g.astype(jnp.float32)
    o = o * jax.nn.silu(z.astype(jnp.float32).reshape(B, L, H, dh))
    return o.reshape(B, L, H * dh).astype(z.dtype) @ w_out


def delta_mixer(hx, hc, w_in, conv_w, a_log, dt_bias, norm_g, w_out, ctx_out):
    qx, kx, vx, zx, gx, bx = dn_project(hx, w_in, conv_w, a_log, dt_bias)
    qc, kc, vc, zc, gc, bc = dn_project(hc, w_in, conv_w, a_log, dt_bias)
    B = hx.shape[0]
    s0 = jnp.zeros((B, DN_HEADS, DN_HEAD_DIM, DN_HEAD_DIM), jnp.float32)
    flip = lambda t: jnp.flip(t, axis=2)
    oc_f, sc_f = gated_delta_chunk(qc, kc, vc, bc[0], gc[0], s0, ctx_out)
    ox_f, _ = gated_delta_chunk(qx, kx, vx, bx[0], gx[0], sc_f, True)
    oc_b, sc_b = gated_delta_chunk(flip(qc), flip(kc), flip(vc), flip(bc[1]), flip(gc[1]), s0, ctx_out)
    ox_b, _ = gated_delta_chunk(flip(qx), flip(kx), flip(vx), flip(bx[1]), flip(gx[1]), sc_b, True)
    yx = dn_output(ox_f + flip(ox_b), zx, norm_g, w_out)
    yc = dn_output(oc_f + flip(oc_b), zc, norm_g, w_out) if ctx_out else None
    return yx, yc


def peer(h, wq, keys, u_tab, v_tab):
    B, L, D = h.shape
    T = B * L
    hf = h.reshape(T, D)
    q = (hf @ wq).reshape(T, PEER_HEADS, 2, PEER_KEY_DIM // 2)
    s = jnp.einsum('thpd,hpkd->thpk', q, keys).astype(jnp.float32)
    s1, i1 = lax.top_k(s[:, :, 0], PEER_TOPK)
    s2, i2 = lax.top_k(s[:, :, 1], PEER_TOPK)
    cand = (s1[..., :, None] + s2[..., None, :]).reshape(T, PEER_HEADS, PEER_TOPK * PEER_TOPK)
    cidx = (i1[..., :, None] * PEER_N_KEYS + i2[..., None, :]).reshape(T, PEER_HEADS, PEER_TOPK * PEER_TOPK)
    sc, pos = lax.top_k(cand, PEER_TOPK)
    nb = T // PEER_BLOCK
    eidx = jnp.take_along_axis(cidx, pos, axis=-1).reshape(nb, PEER_BLOCK, PEER_HEADS * PEER_TOPK)
    gate = jax.nn.softmax(sc, axis=-1).astype(h.dtype).reshape(nb, PEER_BLOCK, PEER_HEADS * PEER_TOPK)
    xb = hf.reshape(nb, PEER_BLOCK, D)

    def block(args):
        xt, et, gt = args
        act = jax.nn.gelu(jnp.einsum('td,ted->te', xt, jnp.take(u_tab, et, axis=0)), approximate=False)
        return jnp.einsum('te,ted->td', gt * act, jnp.take(v_tab, et, axis=0))

    y = lax.map(block, (xb, eidx, gate))
    return y.reshape(B, L, D)


def setup_inputs(seed: int = 0) -> dict:
    key = jax.random.key(seed)
    ks = jax.random.split(key, 24)
    D = D_MODEL
    f32 = jnp.float32
    nrm = lambda k, shape, s: jax.random.normal(k, shape, f32) * s
    dt = jnp.exp(jax.random.uniform(ks[13], (N_DELTA_LAYERS, 2, DN_HEADS), f32,
                                    minval=float(np.log(1e-3)), maxval=float(np.log(1e-1))))
    return {
        "x": nrm(ks[0], (BATCH, SEQ, D), 1.0),
        "c": nrm(ks[1], (BATCH, D), 1.0),
        "ctx": nrm(ks[2], (BATCH, CTX_LEN, D), 1.0),
        "c_ctx": nrm(ks[3], (D,), 1.0),
        "mod_w": nrm(ks[4], (DEPTH, D, 6 * D), 0.5 * D ** -0.5),
        "mod_b": nrm(ks[5], (DEPTH, 6 * D), 0.02),
        "norm1_g": 1.0 + nrm(ks[6], (DEPTH, D), 0.02),
        "norm2_g": 1.0 + nrm(ks[7], (DEPTH, D), 0.02),
        "pool_w": nrm(ks[8], (N_POOL_LAYERS, N_POOL_GROUPS, POOL_GROUP, POOL_GROUP), POOL_GROUP ** -0.5),
        "pool_scale": 1.0 + nrm(ks[9], (N_POOL_LAYERS, D), 0.1),
        "dn_w_in": nrm(ks[10], (N_DELTA_LAYERS, D, 4 * D + 4 * DN_HEADS), D ** -0.5),
        "dn_conv_w": nrm(ks[11], (N_DELTA_LAYERS, DN_CONV, 3 * D), DN_CONV ** -0.5),
        "dn_a_log": jnp.log(jax.random.uniform(ks[12], (N_DELTA_LAYERS, 2, DN_HEADS), f32, minval=1.0, maxval=16.0)),
        "dn_dt_bias": dt + jnp.log(-jnp.expm1(-dt)),
        "dn_norm_g": 1.0 + nrm(ks[14], (N_DELTA_LAYERS, DN_HEAD_DIM), 0.02),
        "dn_w_out": nrm(ks[15], (N_DELTA_LAYERS, D, D), D ** -0.5),
        "peer_wq": nrm(ks[16], (DEPTH, D, PEER_HEADS * PEER_KEY_DIM), D ** -0.5),
        "peer_keys": nrm(ks[17], (DEPTH, PEER_HEADS, 2, PEER_N_KEYS, PEER_KEY_DIM // 2), (PEER_KEY_DIM // 2) ** -0.5),
        "peer_u": nrm(ks[18], (DEPTH, PEER_N_EXPERTS, D), D ** -0.5),
        "peer_v": nrm(ks[19], (DEPTH, PEER_N_EXPERTS, D), 0.5),
        "final_g": 1.0 + nrm(ks[20], (D,), 0.02),
    }


def reference(x, c, ctx, c_ctx, mod_w, mod_b, norm1_g, norm2_g, pool_w, pool_scale,
              dn_w_in, dn_conv_w, dn_a_log, dn_dt_bias, dn_norm_g, dn_w_out,
              peer_wq, peer_keys, peer_u, peer_v, final_g):
    hx, hc = x, ctx
    for i in range(DEPTH):
        last = i == DEPTH - 1
        is_pool = (i % N_MIXERS) == 0
        j = i // N_MIXERS
        mx = [m[:, None, :] for m in jnp.split(jax.nn.silu(c) @ mod_w[i] + mod_b[i], 6, axis=-1)]
        need_ctx = (not last) or (not is_pool)
        if need_ctx:
            mc = jnp.split(jax.nn.silu(c_ctx) @ mod_w[i] + mod_b[i], 6, axis=-1)
        ax = modulate(hx, norm1_g[i], mx[0], mx[1])
        if is_pool:
            hx = hx + mx[2] * pool_mixer(ax, pool_w[j], pool_scale[j], True)
            if not last:
                ac = modulate(hc, norm1_g[i], mc[0], mc[1])
                hc = hc + mc[2] * pool_mixer(ac, pool_w[j], pool_scale[j], False)
        else:
            ac = modulate(hc, norm1_g[i], mc[0], mc[1])
            yx, yc = delta_mixer(ax, ac, dn_w_in[j], dn_conv_w[j], dn_a_log[j], dn_dt_bias[j],
                                 dn_norm_g[j], dn_w_out[j], not last)
            hx = hx + mx[2] * yx
            if not last:
                hc = hc + mc[2] * yc
        hx = hx + mx[5] * peer(modulate(hx, norm2_g[i], mx[3], mx[4]),
                               peer_wq[i], peer_keys[i], peer_u[i], peer_v[i])
        if not last:
            hc = hc + mc[5] * peer(modulate(hc, norm2_g[i], mc[3], mc[4]),
                                   peer_wq[i], peer_keys[i], peer_u[i], peer_v[i])
    return rmsnorm(hx, final_g)
```

```python
import functools

import numpy as np
import jax
import jax.numpy as jnp
from jax import lax
from jax.experimental import pallas as pl
from jax.experimental.pallas import tpu as pltpu

F32 = jnp.float32
BF16 = jnp.bfloat16

D_MODEL = 1024
DEPTH = 2
GRID_W = 64
NORM_EPS = 1e-6
POOL_WINDOWS = (2, 4, 8, 16)
POOL_GROUP = 256
DN_HEADS = 8
DN_HEAD_DIM = 128
DN_CONV = 5
PEER_HEADS = 8
PEER_N_KEYS = 128
PEER_N_EXPERTS = PEER_N_KEYS * PEER_N_KEYS
PEER_TOPK = 16

VMEM_LIMIT_BYTES = 56 * 1024 * 1024

POOL_TILE = 512
ROUTE_TILE = 256
PEER_TOKENS = 512
PEER_EXPERTS = 512
DN_CHUNK = 64
DN_BLOCK = 512
DN_SCAN_HEADS = 4
OUT_TILE = 512

_SQRT_HALF = float(np.sqrt(0.5))


def _cparams(*sem):
    return pltpu.CompilerParams(dimension_semantics=sem, vmem_limit_bytes=VMEM_LIMIT_BYTES)


def _dot(a, b):
    return jnp.dot(a, b, preferred_element_type=F32)


def _split(a):
    hi = a.astype(BF16)
    lo = (a - hi.astype(F32)).astype(BF16)
    return hi, lo


def _dot3(a, b):
    ah, al = _split(a)
    bh, bl = _split(b)
    return _dot(ah, bh) + _dot(ah, bl) + _dot(al, bh)


def _modulate(x, gain, shift, scale):
    ms = jnp.mean(x * x, axis=-1, keepdims=True)
    y = x * lax.rsqrt(ms + NORM_EPS)
    return (y * gain) * (1.0 + scale) + shift


def _silu(x):
    return x * jax.nn.sigmoid(x)


def _mod_kernel(c_ref, w_ref, b_ref, o_ref):
    o_ref[0] = _dot3(_silu(c_ref[...]), w_ref[0]) + b_ref[0]


def _mod_vectors(cc, mod_w, mod_b):
    n = cc.shape[0]
    nb = mod_w.shape[-1] // D_MODEL
    return pl.pallas_call(
        _mod_kernel,
        grid=(DEPTH, nb),
        in_specs=[
            pl.BlockSpec((n, D_MODEL), lambda l, j: (0, 0)),
            pl.BlockSpec((1, D_MODEL, D_MODEL), lambda l, j: (l, 0, j)),
            pl.BlockSpec((1, 1, D_MODEL), lambda l, j: (l, 0, j)),
        ],
        out_specs=pl.BlockSpec((1, n, D_MODEL), lambda l, j: (l, 0, j)),
        out_shape=jax.ShapeDtypeStruct((DEPTH, n, nb * D_MODEL), F32),
        compiler_params=_cparams("arbitrary", "arbitrary"),
        name="mod_vectors",
    )(cc, mod_w, mod_b.reshape(DEPTH, 1, -1))


def _pool_constants_grid(rows):
    tr = POOL_TILE // GRID_W
    s = np.zeros((len(POOL_WINDOWS), POOL_TILE, 3 * POOL_TILE), np.float32)
    invc = np.zeros((rows * GRID_W, 128), np.float32)
    for gi, w in enumerate(POOL_WINDOWS):
        lo, hi = -(w // 2), w - w // 2
        for r in range(tr):
            for c in range(GRID_W):
                c0, c1 = max(c + lo, 0), min(c + hi, GRID_W)
                for r2 in range(r + lo, r + hi):
                    s[gi, r * GRID_W + c, POOL_TILE + r2 * GRID_W + c0:POOL_TILE + r2 * GRID_W + c1] = 1.0
        for r in range(rows):
            nr = min(r + hi, rows) - max(r + lo, 0)
            for c in range(GRID_W):
                nc = min(c + hi, GRID_W) - max(c + lo, 0)
                invc[r * GRID_W + c, gi] = 1.0 / (nr * nc)
    return s, invc


def _pool_constants_seq(n):
    s = np.zeros((len(POOL_WINDOWS), n, n), np.float32)
    invc = np.zeros((n, 128), np.float32)
    for gi, w in enumerate(POOL_WINDOWS):
        lo, hi = -(w // 2), w - w // 2
        for t in range(n):
            t0, t1 = max(t + lo, 0), min(t + hi, n)
            s[gi, t, t0:t1] = 1.0
            invc[t, gi] = 1.0 / (t1 - t0)
    return s, invc


def _pool_k_range(w):
    h = w // 2
    lo = POOL_TILE - h * GRID_W
    hi = POOL_TILE + (POOL_TILE // GRID_W - 1 + h) * GRID_W
    lo = (lo // 128) * 128
    hi = min(-(-hi // 128) * 128, 3 * POOL_TILE)
    return lo, hi


def _pool_finish(x, mod, sums, a_cur, invc_ref, pw_ref, ps_ref, o_ref):
    ys = []
    for gi in range(len(POOL_WINDOWS)):
        cols = slice(gi * POOL_GROUP, (gi + 1) * POOL_GROUP)
        pooled = sums[gi] * invc_ref[:, gi:gi + 1]
        p = pooled - a_cur[:, cols]
        ys.append(_dot3(p, pw_ref[gi]))
    y = jnp.concatenate(ys, axis=-1) * ps_ref[...]
    o_ref[0] = x + mod[2:3] * y


def _pool_grid_kernel(xp_ref, xc_ref, xn_ref, mod_ref, g_ref, s_ref, invc_ref, pw_ref, ps_ref, o_ref):
    t = pl.program_id(1)
    nt = pl.num_programs(1)
    mod = mod_ref[0]
    gain = g_ref[...]
    x = xc_ref[0]
    a_cur = _modulate(x, gain, mod[0:1], mod[1:2])
    a_prev = jnp.where(t > 0, _modulate(xp_ref[0], gain, mod[0:1], mod[1:2]), 0.0)
    a_next = jnp.where(t < nt - 1, _modulate(xn_ref[0], gain, mod[0:1], mod[1:2]), 0.0)
    parts = [_split(a_prev), _split(a_cur), _split(a_next)]
    sums = []
    for gi, w in enumerate(POOL_WINDOWS):
        cols = slice(gi * POOL_GROUP, (gi + 1) * POOL_GROUP)
        k0, k1 = _pool_k_range(w)
        acc = None
        for pi in range(3):
            r0 = max(k0, pi * POOL_TILE)
            r1 = min(k1, (pi + 1) * POOL_TILE)
            if r1 <= r0:
                continue
            sm = s_ref[gi, :, r0:r1]
            rows = slice(r0 - pi * POOL_TILE, r1 - pi * POOL_TILE)
            for part in parts[pi]:
                d = _dot(sm, part[rows, cols])
                acc = d if acc is None else acc + d
        sums.append(acc)
    _pool_finish(x, mod, sums, a_cur, invc_ref, pw_ref, ps_ref, o_ref)


def _pool_seq_kernel(xc_ref, mod_ref, g_ref, s_ref, invc_ref, pw_ref, ps_ref, o_ref):
    mod = mod_ref[0]
    x = xc_ref[0]
    a_cur = _modulate(x, g_ref[...], mod[0:1], mod[1:2])
    hi, lo = _split(a_cur)
    sums = []
    for gi in range(len(POOL_WINDOWS)):
        cols = slice(gi * POOL_GROUP, (gi + 1) * POOL_GROUP)
        sums.append(_dot(s_ref[gi], hi[:, cols]) + _dot(s_ref[gi], lo[:, cols]))
    _pool_finish(x, mod, sums, a_cur, invc_ref, pw_ref, ps_ref, o_ref)


def _pool_mixer_grid(hx, mod, gain, pool_w, pool_scale):
    b, l, d = hx.shape
    nt = l // POOL_TILE
    s_np, invc_np = _pool_constants_grid(l // GRID_W)
    s = jnp.asarray(s_np, BF16)
    invc = jnp.asarray(invc_np)
    ng = len(POOL_WINDOWS)
    xspec = lambda f: pl.BlockSpec((1, POOL_TILE, d), f)
    return pl.pallas_call(
        _pool_grid_kernel,
        grid=(b, nt),
        in_specs=[
            xspec(lambda i, t: (i, jnp.maximum(t - 1, 0), 0)),
            xspec(lambda i, t: (i, t, 0)),
            xspec(lambda i, t: (i, jnp.minimum(t + 1, nt - 1), 0)),
            pl.BlockSpec((1, 8, d), lambda i, t: (i, 0, 0)),
            pl.BlockSpec((1, d), lambda i, t: (0, 0)),
            pl.BlockSpec((ng, POOL_TILE, 3 * POOL_TILE), lambda i, t: (0, 0, 0)),
            pl.BlockSpec((POOL_TILE, 128), lambda i, t: (t, 0)),
            pl.BlockSpec((ng, POOL_GROUP, POOL_GROUP), lambda i, t: (0, 0, 0)),
            pl.BlockSpec((1, d), lambda i, t: (0, 0)),
        ],
        out_specs=pl.BlockSpec((1, POOL_TILE, d), lambda i, t: (i, t, 0)),
        out_shape=jax.ShapeDtypeStruct(hx.shape, F32),
        compiler_params=_cparams("arbitrary", "arbitrary"),
        name="pool_grid",
    )(hx, hx, hx, mod, gain, s, invc, pool_w, pool_scale)


def _pool_mixer_seq(hc, mod, gain, pool_w, pool_scale):
    b, l, d = hc.shape
    s_np, invc_np = _pool_constants_seq(l)
    s = jnp.asarray(s_np, BF16)
    invc = jnp.asarray(invc_np)
    ng = len(POOL_WINDOWS)
    return pl.pallas_call(
        _pool_seq_kernel,
        grid=(b,),
        in_specs=[
            pl.BlockSpec((1, l, d), lambda i: (i, 0, 0)),
            pl.BlockSpec((1, 8, d), lambda i: (0, 0, 0)),
            pl.BlockSpec((1, d), lambda i: (0, 0)),
            pl.BlockSpec((ng, l, l), lambda i: (0, 0, 0)),
            pl.BlockSpec((l, 128), lambda i: (0, 0)),
            pl.BlockSpec((ng, POOL_GROUP, POOL_GROUP), lambda i: (0, 0, 0)),
            pl.BlockSpec((1, d), lambda i: (0, 0)),
        ],
        out_specs=pl.BlockSpec((1, l, d), lambda i: (i, 0, 0)),
        out_shape=jax.ShapeDtypeStruct(hc.shape, F32),
        compiler_params=_cparams("arbitrary"),
        name="pool_seq",
    )(hc, mod, gain, s, invc, pool_w, pool_scale)


def _bitonic_merge(x):
    n = len(x)
    if n == 1:
        return x
    h = n // 2
    hi = [jnp.maximum(x[i], x[i + h]) for i in range(h)]
    lo = [jnp.minimum(x[i], x[i + h]) for i in range(h)]
    return _bitonic_merge(hi) + _bitonic_merge(lo)


def _sort_desc(x):
    n = len(x)
    if n == 1:
        return x
    return _bitonic_merge(_sort_desc(x[:n // 2]) + _sort_desc(x[n // 2:])[::-1])


def _top_merge(a, b):
    n = len(a)
    return _bitonic_merge([jnp.maximum(a[i], b[n - 1 - i]) for i in range(n)])


def _top16_rows(s):
    x = _sort_desc([s[i * 8:(i + 1) * 8] for i in range(PEER_TOPK)])
    for shift in (4, 2, 1):
        x = _top_merge(x, [pltpu.roll(v, shift, axis=0) for v in x])
    return x


def _top16_pairs(a, b, op, pad):
    c = lambda i, j: op(a[i - 1], b[j - 1])
    x = [c(1, j) for j in range(1, 17)]
    y = _bitonic_merge([c(2, j) for j in range(1, 9)] + [c(i, 1) for i in range(16, 8, -1)])
    z = _sort_desc([c(i, j) for i, nj in ((3, 5), (4, 4), (5, 3), (6, 2), (7, 2)) for j in range(1, nj + 1)])
    w = [c(8, 1), c(8, 2)] + [pad] * 14
    return _top_merge(_top_merge(_top_merge(x, y), z), w)


def _route_kernel(h_ref, mod_ref, g_ref, wqh_ref, wql_ref, keys_ref, hmt_ref, e1_ref, e2_ref, th_ref):
    mod = mod_ref[0]
    hm = _modulate(h_ref[...], g_ref[...], mod[3:4], mod[4:5])
    hmt = hm.T
    hi, lo = _split(hmt)
    hmt_ref[...] = hi
    wqh = wqh_ref[...]
    qt = _dot(wqh, hi) + _dot(wqh, lo) + _dot(wql_ref[...], hi)
    n = qt.shape[1]
    scores, tops = [], []
    for hp in range(2 * PEER_HEADS):
        s = _dot3(keys_ref[hp], qt[hp * PEER_N_KEYS:(hp + 1) * PEER_N_KEYS])
        scores.append(s)
        tops.append(_top16_rows(s))
    sub = lax.broadcasted_iota(jnp.int32, (8, n), 0)

    def stack_heads(p, i):
        out = tops[p][i]
        for h in range(1, PEER_HEADS):
            out = jnp.where(sub == h, tops[2 * h + p][i], out)
        return out

    a = [stack_heads(0, i) for i in range(PEER_TOPK)]
    b = [stack_heads(1, i) for i in range(PEER_TOPK)]
    csum = _top16_pairs(a, b, lambda u, v: u + v, jnp.full((8, n), -jnp.inf, F32))
    z = jnp.exp(csum[1] - csum[0]) + 1.0
    for kk in range(2, PEER_TOPK):
        z = z + jnp.exp(csum[kk] - csum[0])
    rz = 1.0 / z
    en = [jnp.exp(a[i] - a[0]) * rz for i in range(PEER_TOPK)]
    eb = [jnp.exp(b[i] - b[0]) for i in range(PEER_TOPK)]
    cprod = _top16_pairs(en, eb, lambda u, v: u * v, jnp.full((8, n), -1.0, F32))
    th_ref[...] = cprod[PEER_TOPK - 1]
    for h in range(PEER_HEADS):
        a1 = tops[2 * h][0][0:1]
        b1 = tops[2 * h + 1][0][0:1]
        e1_ref[h] = jnp.exp(scores[2 * h] - a1) * rz[h:h + 1]
        e2_ref[h] = jnp.exp(scores[2 * h + 1] - b1)


def _peer_route(h2d, mod, gain, wqt_hi, wqt_lo, keys, tiles_per_mod):
    t, d = h2d.shape
    n = ROUTE_TILE
    nq = wqt_hi.shape[0]
    modmap = (lambda i: (i // tiles_per_mod, 0, 0)) if tiles_per_mod else (lambda i: (0, 0, 0))
    return pl.pallas_call(
        _route_kernel,
        grid=(t // n,),
        in_specs=[
            pl.BlockSpec((n, d), lambda i: (i, 0)),
            pl.BlockSpec((1, 8, d), modmap),
            pl.BlockSpec((1, d), lambda i: (0, 0)),
            pl.BlockSpec((nq, d), lambda i: (0, 0)),
            pl.BlockSpec((nq, d), lambda i: (0, 0)),
            pl.BlockSpec((2 * PEER_HEADS, PEER_N_KEYS, PEER_N_KEYS), lambda i: (0, 0, 0)),
        ],
        out_specs=[
            pl.BlockSpec((d, n), lambda i: (0, i)),
            pl.BlockSpec((PEER_HEADS, PEER_N_KEYS, n), lambda i: (0, 0, i)),
            pl.BlockSpec((PEER_HEADS, PEER_N_KEYS, n), lambda i: (0, 0, i)),
            pl.BlockSpec((PEER_HEADS, n), lambda i: (0, i)),
        ],
        out_shape=[
            jax.ShapeDtypeStruct((d, t), BF16),
            jax.ShapeDtypeStruct((PEER_HEADS, PEER_N_KEYS, t), F32),
            jax.ShapeDtypeStruct((PEER_HEADS, PEER_N_KEYS, t), F32),
            jax.ShapeDtypeStruct((PEER_HEADS, t), F32),
        ],
        compiler_params=_cparams("arbitrary"),
        name="peer_route",
    )(h2d, mod, gain, wqt_hi, wqt_lo, keys)


def _peer_kernel(*refs, final):
    if final:
        h_ref, mod_ref, hmt_ref, e1_ref, e2_ref, th_ref, u_ref, vt_ref, fg_ref, o_ref, acc_ref, ga_ref = refs
    else:
        h_ref, mod_ref, hmt_ref, e1_ref, e2_ref, th_ref, u_ref, vt_ref, o_ref, acc_ref, ga_ref = refs
    j = pl.program_id(1)
    nchunk = PEER_EXPERTS // PEER_N_KEYS

    @pl.when(j == 0)
    def _():
        acc_ref[...] = jnp.zeros_like(acc_ref)

    act = _dot(u_ref[...], hmt_ref[...])
    for c in range(nchunk):
        i1 = j * nchunk + c
        g = None
        for h in range(PEER_HEADS):
            p = e1_ref[h, pl.ds(i1, 1), :] * e2_ref[h]
            sel = jnp.where(p >= th_ref[h:h + 1, :], p, 0.0)
            g = sel if g is None else g + sel
        a = act[c * PEER_N_KEYS:(c + 1) * PEER_N_KEYS]
        gelu = 0.5 * a * (1.0 + lax.erf(a * _SQRT_HALF))
        ga_ref[c * PEER_N_KEYS:(c + 1) * PEER_N_KEYS, :] = (g * gelu).astype(BF16)
    acc_ref[...] += _dot(vt_ref[...], ga_ref[...])

    @pl.when(j == pl.num_programs(1) - 1)
    def _():
        res = h_ref[...] + mod_ref[0][5:6] * acc_ref[...].T
        if final:
            ms = jnp.mean(res * res, axis=-1, keepdims=True)
            res = res * lax.rsqrt(ms + NORM_EPS) * fg_ref[...]
        o_ref[...] = res


def _peer_experts(h2d, mod, hmt, e1, e2, th, u_bf, vt_bf, tiles_per_mod, final_g=None):
    t, d = h2d.shape
    n, eb = PEER_TOKENS, PEER_EXPERTS
    modmap = (lambda i, j: (i // tiles_per_mod, 0, 0)) if tiles_per_mod else (lambda i, j: (0, 0, 0))
    in_specs = [
        pl.BlockSpec((n, d), lambda i, j: (i, 0)),
        pl.BlockSpec((1, 8, d), modmap),
        pl.BlockSpec((d, n), lambda i, j: (0, i)),
        pl.BlockSpec((PEER_HEADS, PEER_N_KEYS, n), lambda i, j: (0, 0, i)),
        pl.BlockSpec((PEER_HEADS, PEER_N_KEYS, n), lambda i, j: (0, 0, i)),
        pl.BlockSpec((PEER_HEADS, n), lambda i, j: (0, i)),
        pl.BlockSpec((eb, d), lambda i, j: (j, 0)),
        pl.BlockSpec((d, eb), lambda i, j: (0, j)),
    ]
    args = [h2d, mod, hmt, e1, e2, th, u_bf, vt_bf]
    if final_g is not None:
        in_specs.append(pl.BlockSpec((1, d), lambda i, j: (0, 0)))
        args.append(final_g)
    return pl.pallas_call(
        functools.partial(_peer_kernel, final=final_g is not None),
        grid=(t // n, PEER_N_EXPERTS // eb),
        in_specs=in_specs,
        out_specs=pl.BlockSpec((n, d), lambda i, j: (i, 0)),
        out_shape=jax.ShapeDtypeStruct((t, d), F32),
        scratch_shapes=[pltpu.VMEM((d, n), F32), pltpu.VMEM((eb, n), BF16)],
        compiler_params=_cparams("arbitrary", "arbitrary"),
        name="peer_experts",
    )(*args)


def _peer_layer(h, mod, gain, wq, keys, u_tab, v_tab, per_batch_mod, final_g=None):
    b, l, d = h.shape
    h2d = h.reshape(b * l, d)
    wqt = wq.T
    wqt_hi = wqt.astype(BF16)
    wqt_lo = (wqt - wqt_hi.astype(F32)).astype(BF16)
    keys2 = keys.reshape(2 * PEER_HEADS, PEER_N_KEYS, keys.shape[-1])
    hmt, e1, e2, th = _peer_route(h2d, mod, gain, wqt_hi, wqt_lo, keys2,
                                  l // ROUTE_TILE if per_batch_mod else 0)
    out = _peer_experts(h2d, mod, hmt, e1, e2, th, u_tab.astype(BF16), v_tab.T.astype(BF16),
                        l // PEER_TOKENS if per_batch_mod else 0, final_g)
    return out.reshape(b, l, d)


def _softplus(x):
    return jnp.maximum(x, 0.0) + jnp.log1p(jnp.exp(-jnp.abs(x)))


def _dnproj_kernel(h_ref, mod_ref, g_ref, wq_ref, wk_ref, wv_ref, wz_ref, cq_ref, ck_ref, cv_ref,
                   wab_ref, abp_ref, q_ref, k_ref, v_ref, z_ref, gb_ref, ax_ref):
    hb = pl.program_id(1)

    @pl.when(hb == 0)
    def _():
        mod = mod_ref[0]
        axb = _modulate(h_ref[0], g_ref[...], mod[0:1], mod[1:2]).astype(BF16)
        ax_ref[...] = axb
        ab = _dot(axb, wab_ref[...])
        prm = abp_ref[...]
        gcol = -jnp.exp(prm[0:1]) * _softplus(ab + prm[1:2])
        gb_ref[0] = jnp.where(prm[2:3] > 0.5, gcol, jax.nn.sigmoid(ab))

    axb = ax_ref[...]
    n = axb.shape[0]
    row = lax.broadcasted_iota(jnp.int32, (n, 2 * DN_HEAD_DIM), 0)

    def conv_silu(p, cw):
        y = p * cw[DN_CONV // 2:DN_CONV // 2 + 1]
        for kk in range(DN_CONV):
            dlt = kk - DN_CONV // 2
            if dlt == 0:
                continue
            shifted = pltpu.roll(p, (-dlt) % n, axis=0)
            valid = (row + dlt >= 0) & (row + dlt < n)
            y = y + jnp.where(valid, shifted, 0.0) * cw[kk:kk + 1]
        return _silu(y)

    def heads(u, ref, normalise, scale):
        for hh in range(2):
            uh = u[:, hh * DN_HEAD_DIM:(hh + 1) * DN_HEAD_DIM]
            if normalise:
                uh = uh * lax.rsqrt(jnp.sum(uh * uh, axis=-1, keepdims=True) + NORM_EPS)
            ref[0, hh] = uh * scale if scale != 1.0 else uh

    heads(conv_silu(_dot(axb, wq_ref[...]), cq_ref[...]), q_ref, True, DN_HEAD_DIM ** -0.5)
    heads(conv_silu(_dot(axb, wk_ref[...]), ck_ref[...]), k_ref, True, 1.0)
    heads(conv_silu(_dot(axb, wv_ref[...]), cv_ref[...]), v_ref, False, 1.0)
    z_ref[0] = _dot(axb, wz_ref[...])


def _dn_project(h, mod, gain, w4, conv_w, wab, abp, per_batch_mod):
    b, l, d = h.shape
    nhb = DN_HEADS // 2
    cw = 2 * DN_HEAD_DIM
    modmap = (lambda i, j: (i, 0, 0)) if per_batch_mod else (lambda i, j: (0, 0, 0))
    wspec = lambda off: pl.BlockSpec((d, cw), lambda i, j: (0, off * nhb + j))
    cspec = lambda off: pl.BlockSpec((DN_CONV, cw), lambda i, j: (0, off * nhb + j))
    hspec = pl.BlockSpec((1, 2, l, DN_HEAD_DIM), lambda i, j: (i, j, 0, 0))
    hshape = jax.ShapeDtypeStruct((b, DN_HEADS, l, DN_HEAD_DIM), F32)
    return pl.pallas_call(
        _dnproj_kernel,
        grid=(b, nhb),
        in_specs=[
            pl.BlockSpec((1, l, d), lambda i, j: (i, 0, 0)),
            pl.BlockSpec((1, 8, d), modmap),
            pl.BlockSpec((1, d), lambda i, j: (0, 0)),
            wspec(0), wspec(1), wspec(2), wspec(3),
            cspec(0), cspec(1), cspec(2),
            pl.BlockSpec((d, 128), lambda i, j: (0, 0)),
            pl.BlockSpec((8, 128), lambda i, j: (0, 0)),
        ],
        out_specs=[hspec, hspec, hspec,
                   pl.BlockSpec((1, l, cw), lambda i, j: (i, 0, j)),
                   pl.BlockSpec((1, l, 128), lambda i, j: (i, 0, 0))],
        out_shape=[hshape, hshape, hshape,
                   jax.ShapeDtypeStruct((b, l, d), F32),
                   jax.ShapeDtypeStruct((b, l, 128), F32)],
        scratch_shapes=[pltpu.VMEM((l, d), BF16)],
        compiler_params=_cparams("arbitrary", "arbitrary"),
        name="dn_project",
    )(h, mod, gain, w4, w4, w4, w4, conv_w, conv_w, conv_w, wab, abp)


def _dnprep_kernel(q_ref, k_ref, v_ref, gr_ref, gc_ref, u_ref, w_ref, kg_ref, qg_ref, at_ref, eg_ref,
                   *, with_output):
    c = DN_CHUNK
    nchunks = q_ref.shape[2] // c
    ri = lax.broadcasted_iota(jnp.int32, (c, c), 0)
    ci = lax.broadcasted_iota(jnp.int32, (c, c), 1)
    for ch in range(nchunks):
        rows = slice(ch * c, (ch + 1) * c)
        k = k_ref[0, 0, rows, :]
        v = v_ref[0, 0, rows, :]
        kb16 = k.astype(BF16)
        if with_output:
            q = q_ref[0, 0, rows, :]
            qk = lax.dot_general(q.astype(BF16), kb16, (((1,), (1,)), ((), ())), preferred_element_type=F32)
        grow = gr_ref[0, 0, ch]
        gcol = gc_ref[0, 0, rows, :]
        for dr in range(2):
            g_r = grow[2 * dr:2 * dr + 1, :]
            g_c = gcol[:, 2 * dr:2 * dr + 1]
            beta_c = gcol[:, 2 * dr + 1:2 * dr + 2]
            incl = (ri >= ci) if dr == 0 else (ri <= ci)
            incl_t = (ri <= ci) if dr == 0 else (ri >= ci)
            strict = (ri > ci) if dr == 0 else (ri < ci)
            cum_c = jnp.sum(jnp.where(incl, g_r, 0.0), axis=1, keepdims=True)
            cum_r = jnp.sum(jnp.where(incl_t, g_c, 0.0), axis=0, keepdims=True)
            tot = jnp.sum(g_r, axis=1, keepdims=True)
            decay = jnp.where(incl, jnp.exp(jnp.where(incl, cum_c - cum_r, 0.0)), 0.0)
            kb = k * beta_c
            vb = v * beta_c
            kk = lax.dot_general(kb.astype(BF16), kb16, (((1,), (1,)), ((), ())), preferred_element_type=F32)
            amat = jnp.where(strict, kk * decay, 0.0)
            p = -amat
            r = p
            for _ in range(5):
                p16 = p.astype(BF16)
                p = _dot(p16, p16)
                r = r + p + _dot(r.astype(BF16), p.astype(BF16))
            r16 = r.astype(BF16)
            ecum = jnp.exp(cum_c)
            kbg = kb * ecum
            u_ref[0, 0, dr, rows, :] = vb + _dot(r16, vb.astype(BF16))
            w_ref[0, 0, dr, rows, :] = (kbg + _dot(r16, kbg.astype(BF16))).astype(BF16)
            kg_ref[0, 0, dr, rows, :] = (k * jnp.exp(tot - cum_c)).astype(BF16)
            eg_ref[0, 0, dr, ch] = jnp.broadcast_to(jnp.exp(tot), (8, DN_HEAD_DIM))
            if with_output:
                qg_ref[0, 0, dr, rows, :] = (q * ecum).astype(BF16)
                at_ref[0, 0, dr, rows, :] = jnp.where(incl, qk * decay, 0.0).astype(BF16)
            else:
                qg_ref[0, 0, dr, rows, :] = jnp.zeros((c, DN_HEAD_DIM), BF16)
                at_ref[0, 0, dr, rows, :] = jnp.zeros((c, c), BF16)


def _dn_prepare(q, k, v, g_rows, g_cols, with_output):
    b, hh, l, dh = q.shape
    blk = min(DN_BLOCK, l)
    nb = l // blk
    cpb = blk // DN_CHUNK
    qspec = pl.BlockSpec((1, 1, blk, dh), lambda i, h, t: (i, h, t, 0))
    ospec = lambda w: pl.BlockSpec((1, 1, 2, blk, w), lambda i, h, t: (i, h, 0, t, 0))
    oshape = lambda w, dt: jax.ShapeDtypeStruct((b, hh, 2, l, w), dt)
    return pl.pallas_call(
        functools.partial(_dnprep_kernel, with_output=with_output),
        grid=(b, hh, nb),
        in_specs=[qspec, qspec, qspec,
                  pl.BlockSpec((1, 1, cpb, 8, DN_CHUNK), lambda i, h, t: (i, h, t, 0, 0)),
                  pl.BlockSpec((1, 1, blk, 8), lambda i, h, t: (i, h, t, 0))],
        out_specs=[ospec(dh), ospec(dh), ospec(dh), ospec(dh), ospec(DN_CHUNK),
                   pl.BlockSpec((1, 1, 2, cpb, 8, dh), lambda i, h, t: (i, h, 0, t, 0, 0))],
        out_shape=[oshape(dh, F32), oshape(dh, BF16), oshape(dh, BF16), oshape(dh, BF16),
                   oshape(DN_CHUNK, BF16),
                   jax.ShapeDtypeStruct((b, hh, 2, l // DN_CHUNK, 8, dh), F32)],
        compiler_params=_cparams("arbitrary", "arbitrary", "arbitrary"),
        name="dn_prepare",
    )(q, k, v, g_rows, g_cols)


def _dnscan_kernel(*refs, with_output, has_init):
    idx = 0
    dirs = []
    for _ in range(2):
        dirs.append(refs[idx:idx + 6])
        idx += 6
    if has_init:
        s0_ref = refs[idx]
        idx += 1
    if with_output:
        of_ref, ob_ref = refs[idx:idx + 2]
        idx += 2
    sout_ref = refs[idx]
    s_ref = refs[idx + 1]
    t = pl.program_id(2)
    nh = s_ref.shape[1]
    c = DN_CHUNK
    nchunks = dirs[0][0].shape[3] // c

    @pl.when(t == 0)
    def _():
        if has_init:
            for dr in range(2):
                s_ref[dr] = s0_ref[0, :, dr]
        else:
            s_ref[...] = jnp.zeros_like(s_ref)

    for step in range(nchunks):
        for dr in range(2):
            u_ref, w_ref, kg_ref, qg_ref, at_ref, eg_ref = dirs[dr]
            ch = step if dr == 0 else nchunks - 1 - step
            rows = slice(ch * c, (ch + 1) * c)
            for h in range(nh):
                s = s_ref[dr, h]
                s16 = s.astype(BF16)
                v_new = u_ref[0, h, 0, rows, :] - _dot(w_ref[0, h, 0, rows, :], s16)
                vn16 = v_new.astype(BF16)
                if with_output:
                    o = _dot(qg_ref[0, h, 0, rows, :], s16) + _dot(at_ref[0, h, 0, rows, :], vn16)
                    oref = of_ref if dr == 0 else ob_ref
                    oref[0, rows, h * DN_HEAD_DIM:(h + 1) * DN_HEAD_DIM] = o
                upd = lax.dot_general(kg_ref[0, h, 0, rows, :], vn16, (((0,), (0,)), ((), ())),
                                      preferred_element_type=F32)
                s_ref[dr, h] = s * eg_ref[0, h, 0, ch, 0:1, :] + upd

    @pl.when(t == pl.num_programs(2) - 1)
    def _():
        for dr in range(2):
            sout_ref[0, :, dr] = s_ref[dr]


def _dn_scan(prep, s0, with_output):
    u = prep[0]
    b, hh, _, l, dh = u.shape
    blk = min(DN_BLOCK, l)
    nb = l // blk
    cpb = blk // DN_CHUNK
    nh = DN_SCAN_HEADS
    in_specs, args = [], []

    def dir_map(dr, extra):
        def index_map(i, g, t):
            return (i, g, dr, t if dr == 0 else nb - 1 - t) + (0,) * extra
        return index_map

    for dr in range(2):
        for arr in prep[:5]:
            in_specs.append(pl.BlockSpec((1, nh, 1, blk, arr.shape[-1]), dir_map(dr, 1)))
            args.append(arr)
        in_specs.append(pl.BlockSpec((1, nh, 1, cpb, 8, dh), dir_map(dr, 2)))
        args.append(prep[5])
    sspec = pl.BlockSpec((1, nh, 2, dh, dh), lambda i, g, t: (i, g, 0, 0, 0))
    if s0 is not None:
        in_specs.append(sspec)
        args.append(s0)
    out_specs, out_shape = [], []
    if with_output:
        out_specs += [pl.BlockSpec((1, blk, nh * dh), lambda i, g, t: (i, t, g)),
                      pl.BlockSpec((1, blk, nh * dh), lambda i, g, t: (i, nb - 1 - t, g))]
        out_shape += [jax.ShapeDtypeStruct((b, l, hh * dh), F32)] * 2
    out_specs.append(sspec)
    out_shape.append(jax.ShapeDtypeStruct((b, hh, 2, dh, dh), F32))
    return pl.pallas_call(
        functools.partial(_dnscan_kernel, with_output=with_output, has_init=s0 is not None),
        grid=(b, hh // nh, nb),
        in_specs=in_specs,
        out_specs=out_specs,
        out_shape=out_shape,
        scratch_shapes=[pltpu.VMEM((2, nh, dh, dh), F32)],
        compiler_params=_cparams("arbitrary", "arbitrary", "arbitrary"),
        name="dn_scan",
    )(*args)


def _dnout_kernel(of_ref, ob_ref, z_ref, h_ref, mod_ref, ng_ref, wo_ref, o_ref):
    o = of_ref[0] + ob_ref[0]
    z = z_ref[0]
    parts = []
    for h in range(DN_HEADS):
        cols = slice(h * DN_HEAD_DIM, (h + 1) * DN_HEAD_DIM)
        oh = o[:, cols]
        oh = oh * lax.rsqrt(jnp.mean(oh * oh, axis=-1, keepdims=True) + NORM_EPS) * ng_ref[...]
        parts.append(oh * _silu(z[:, cols]))
    y = _dot(jnp.concatenate(parts, axis=-1).astype(BF16), wo_ref[...])
    o_ref[0] = h_ref[0] + mod_ref[0][2:3] * y


def _dn_output(of, ob, z, h, mod, norm_g, w_out_bf):
    b, l, d = h.shape
    n = OUT_TILE
    tok = pl.BlockSpec((1, n, d), lambda i, t: (i, t, 0))
    return pl.pallas_call(
        _dnout_kernel,
        grid=(b, l // n),
        in_specs=[tok, tok, tok, tok,
                  pl.BlockSpec((1, 8, d), lambda i, t: (i, 0, 0)),
                  pl.BlockSpec((1, DN_HEAD_DIM), lambda i, t: (0, 0)),
                  pl.BlockSpec((d, d), lambda i, t: (0, 0))],
        out_specs=tok,
        out_shape=jax.ShapeDtypeStruct(h.shape, F32),
        compiler_params=_cparams("arbitrary", "arbitrary"),
        name="dn_output",
    )(of, ob, z, h, mod, norm_g, w_out_bf)


def _dn_gate_layouts(gb):
    b, l, _ = gb.shape
    g4 = gb[:, :, :4 * DN_HEADS].reshape(b, l, 2, 2, DN_HEADS)
    g4 = jnp.transpose(g4, (0, 4, 1, 2, 3)).reshape(b, DN_HEADS, l, 4)
    cols = jnp.concatenate([g4, jnp.zeros_like(g4)], axis=-1)
    rows = jnp.transpose(cols.reshape(b, DN_HEADS, l // DN_CHUNK, DN_CHUNK, 8), (0, 1, 2, 4, 3))
    return rows, cols


def _delta_layer(hx, hc, modx, modc, gain, w_in, conv_w, a_log, dt_bias, norm_g, w_out):
    d = D_MODEL
    w4 = w_in[:, :4 * d].astype(BF16)
    nab = 4 * DN_HEADS
    wab = jnp.zeros((d, 128), F32).at[:, :nab].set(w_in[:, 4 * d:]).astype(BF16)
    is_g = jnp.zeros((2, 2, DN_HEADS), F32).at[:, 0].set(1.0).reshape(-1)
    place = lambda p: jnp.zeros((2, 2, DN_HEADS), F32).at[:, 0].set(p).reshape(-1)
    abp = jnp.zeros((8, 128), F32)
    abp = abp.at[0, :nab].set(place(a_log)).at[1, :nab].set(place(dt_bias)).at[2, :nab].set(is_g)

    qx, kx, vx, zx, gbx = _dn_project(hx, modx, gain, w4, conv_w, wab, abp, True)
    qc, kc, vc, _, gbc = _dn_project(hc, modc, gain, w4, conv_w, wab, abp, False)
    prep_c = _dn_prepare(qc, kc, vc, *_dn_gate_layouts(gbc), with_output=False)
    prep_x = _dn_prepare(qx, kx, vx, *_dn_gate_layouts(gbx), with_output=True)
    (s_ctx,) = _dn_scan(prep_c, None, with_output=False)
    of, ob, _ = _dn_scan(prep_x, s_ctx, with_output=True)
    return _dn_output(of, ob, zx, hx, modx, norm_g.reshape(1, -1), w_out.astype(BF16))


def kernel(x, c, ctx, c_ctx, mod_w, mod_b, norm1_g, norm2_g, pool_w, pool_scale, dn_w_in, dn_conv_w,
           dn_a_log, dn_dt_bias, dn_norm_g, dn_w_out, peer_wq, peer_keys, peer_u, peer_v, final_g):
    bsz = x.shape[0]
    d = D_MODEL
    cc = jnp.zeros((16, d), F32).at[:bsz].set(c).at[bsz].set(c_ctx)
    mods = _mod_vectors(cc, mod_w, mod_b).reshape(DEPTH, 16, 6, d)
    mods = jnp.concatenate([mods, jnp.zeros((DEPTH, 16, 2, d), F32)], axis=2)
    modx = [mods[i, :bsz] for i in range(DEPTH)]
    modc = [mods[i, bsz:bsz + 1] for i in range(DEPTH)]
    row = lambda v: v.reshape(1, -1)

    hx = _pool_mixer_grid(x, modx[0], row(norm1_g[0]), pool_w[0], row(pool_scale[0]))
    hc = _pool_mixer_seq(ctx, modc[0], row(norm1_g[0]), pool_w[0], row(pool_scale[0]))
    hx = _peer_layer(hx, modx[0], row(norm2_g[0]), peer_wq[0], peer_keys[0], peer_u[0], peer_v[0], True)
    hc = _peer_layer(hc, modc[0], row(norm2_g[0]), peer_wq[0], peer_keys[0], peer_u[0], peer_v[0], False)

    hx = _delta_layer(hx, hc, modx[1], modc[1], row(norm1_g[1]), dn_w_in[0], dn_conv_w[0], dn_a_log[0],
                      dn_dt_bias[0], dn_norm_g[0], dn_w_out[0])
    hx = _peer_layer(hx, modx[1], row(norm2_g[1]), peer_wq[1], peer_keys[1], peer_u[1], peer_v[1], True,
                     final_g=row(final_g))
    return hx
```

```python
import functools

import numpy as np
import jax
import jax.numpy as jnp
from jax import lax
from jax.experimental import pallas as pl
from jax.experimental.pallas import tpu as pltpu

F32 = jnp.float32
BF16 = jnp.bfloat16

D_MODEL = 1024
DEPTH = 2
GRID_W = 64
NORM_EPS = 1e-6
POOL_WINDOWS = (2, 4, 8, 16)
POOL_GROUP = 256
DN_HEADS = 8
DN_HEAD_DIM = 128
DN_CONV = 5
PEER_HEADS = 8
PEER_N_KEYS = 128
PEER_N_EXPERTS = PEER_N_KEYS * PEER_N_KEYS
PEER_TOPK = 16

VMEM_LIMIT_BYTES = 56 * 1024 * 1024

POOL_TILE = 512
ROUTE_TILE = 256
PEER_TOKENS = 512
PEER_EXPERTS = 512
DN_CHUNK = 64
DN_BLOCK = 512
DN_PROBLEMS = 64
DN_SCAN_HEADS = 4
OUT_TILE = 512

_SQRT_HALF = float(np.sqrt(0.5))


def _cparams(*sem):
    return pltpu.CompilerParams(dimension_semantics=sem, vmem_limit_bytes=VMEM_LIMIT_BYTES)


def _dot(a, b):
    return jnp.dot(a, b, preferred_element_type=F32)


def _split(a):
    hi = a.astype(BF16)
    lo = (a - hi.astype(F32)).astype(BF16)
    return hi, lo


def _dot3(a, b):
    ah, al = _split(a)
    bh, bl = _split(b)
    return _dot(ah, bh) + _dot(ah, bl) + _dot(al, bh)


def _modulate(x, gain, shift, scale):
    ms = jnp.mean(x * x, axis=-1, keepdims=True)
    y = x * lax.rsqrt(ms + NORM_EPS)
    return (y * gain) * (1.0 + scale) + shift


def _silu(x):
    return x * jax.nn.sigmoid(x)


def _mod_kernel(c_ref, w_ref, b_ref, o_ref):
    o_ref[0] = _dot3(_silu(c_ref[...]), w_ref[0]) + b_ref[0]


def _mod_vectors(cc, mod_w, mod_b):
    n = cc.shape[0]
    nb = mod_w.shape[-1] // D_MODEL
    return pl.pallas_call(
        _mod_kernel,
        grid=(DEPTH, nb),
        in_specs=[
            pl.BlockSpec((n, D_MODEL), lambda l, j: (0, 0)),
            pl.BlockSpec((1, D_MODEL, D_MODEL), lambda l, j: (l, 0, j)),
            pl.BlockSpec((1, 1, D_MODEL), lambda l, j: (l, 0, j)),
        ],
        out_specs=pl.BlockSpec((1, n, D_MODEL), lambda l, j: (l, 0, j)),
        out_shape=jax.ShapeDtypeStruct((DEPTH, n, nb * D_MODEL), F32),
        compiler_params=_cparams("arbitrary", "arbitrary"),
        name="mod_vectors",
    )(cc, mod_w, mod_b.reshape(DEPTH, 1, -1))


def _pool_constants_grid(rows):
    tr = POOL_TILE // GRID_W
    s = np.zeros((len(POOL_WINDOWS), POOL_TILE, 3 * POOL_TILE), np.float32)
    invc = np.zeros((rows * GRID_W, 128), np.float32)
    for gi, w in enumerate(POOL_WINDOWS):
        lo, hi = -(w // 2), w - w // 2
        for r in range(tr):
            for c in range(GRID_W):
                c0, c1 = max(c + lo, 0), min(c + hi, GRID_W)
                for r2 in range(r + lo, r + hi):
                    s[gi, r * GRID_W + c, POOL_TILE + r2 * GRID_W + c0:POOL_TILE + r2 * GRID_W + c1] = 1.0
        for r in range(rows):
            nr = min(r + hi, rows) - max(r + lo, 0)
            for c in range(GRID_W):
                nc = min(c + hi, GRID_W) - max(c + lo, 0)
                invc[r * GRID_W + c, gi] = 1.0 / (nr * nc)
    return s, invc


def _pool_constants_seq(n):
    s = np.zeros((len(POOL_WINDOWS), n, n), np.float32)
    invc = np.zeros((n, 128), np.float32)
    for gi, w in enumerate(POOL_WINDOWS):
        lo, hi = -(w // 2), w - w // 2
        for t in range(n):
            t0, t1 = max(t + lo, 0), min(t + hi, n)
            s[gi, t, t0:t1] = 1.0
            invc[t, gi] = 1.0 / (t1 - t0)
    return s, invc


def _pool_k_range(w):
    h = w // 2
    lo = POOL_TILE - h * GRID_W
    hi = POOL_TILE + (POOL_TILE // GRID_W - 1 + h) * GRID_W
    lo = (lo // 128) * 128
    hi = min(-(-hi // 128) * 128, 3 * POOL_TILE)
    return lo, hi


def _pool_finish(x, mod, sums, a_cur, invc_ref, pw_ref, ps_ref, o_ref):
    ys = []
    for gi in range(len(POOL_WINDOWS)):
        cols = slice(gi * POOL_GROUP, (gi + 1) * POOL_GROUP)
        pooled = sums[gi] * invc_ref[:, gi:gi + 1]
        p = pooled - a_cur[:, cols]
        ys.append(_dot3(p, pw_ref[gi]))
    y = jnp.concatenate(ys, axis=-1) * ps_ref[...]
    o_ref[0] = x + mod[2:3] * y


def _pool_grid_kernel(xp_ref, xc_ref, xn_ref, mod_ref, g_ref, s_ref, invc_ref, pw_ref, ps_ref, o_ref):
    t = pl.program_id(1)
    nt = pl.num_programs(1)
    mod = mod_ref[0]
    gain = g_ref[...]
    x = xc_ref[0]
    a_cur = _modulate(x, gain, mod[0:1], mod[1:2])
    a_prev = jnp.where(t > 0, _modulate(xp_ref[0], gain, mod[0:1], mod[1:2]), 0.0)
    a_next = jnp.where(t < nt - 1, _modulate(xn_ref[0], gain, mod[0:1], mod[1:2]), 0.0)
    parts = [_split(a_prev), _split(a_cur), _split(a_next)]
    sums = []
    for gi, w in enumerate(POOL_WINDOWS):
        cols = slice(gi * POOL_GROUP, (gi + 1) * POOL_GROUP)
        k0, k1 = _pool_k_range(w)
        acc = None
        for pi in range(3):
            r0 = max(k0, pi * POOL_TILE)
            r1 = min(k1, (pi + 1) * POOL_TILE)
            if r1 <= r0:
                continue
            sm = s_ref[gi, :, r0:r1]
            rows = slice(r0 - pi * POOL_TILE, r1 - pi * POOL_TILE)
            for part in parts[pi]:
                d = _dot(sm, part[rows, cols])
                acc = d if acc is None else acc + d
        sums.append(acc)
    _pool_finish(x, mod, sums, a_cur, invc_ref, pw_ref, ps_ref, o_ref)


def _pool_seq_kernel(xc_ref, mod_ref, g_ref, s_ref, invc_ref, pw_ref, ps_ref, o_ref):
    mod = mod_ref[0]
    x = xc_ref[0]
    a_cur = _modulate(x, g_ref[...], mod[0:1], mod[1:2])
    hi, lo = _split(a_cur)
    sums = []
    for gi in range(len(POOL_WINDOWS)):
        cols = slice(gi * POOL_GROUP, (gi + 1) * POOL_GROUP)
        sums.append(_dot(s_ref[gi], hi[:, cols]) + _dot(s_ref[gi], lo[:, cols]))
    _pool_finish(x, mod, sums, a_cur, invc_ref, pw_ref, ps_ref, o_ref)


def _pool_mixer_grid(hx, mod, gain, pool_w, pool_scale):
    b, l, d = hx.shape
    nt = l // POOL_TILE
    s_np, invc_np = _pool_constants_grid(l // GRID_W)
    s = jnp.asarray(s_np, BF16)
    invc = jnp.asarray(invc_np)
    ng = len(POOL_WINDOWS)
    xspec = lambda f: pl.BlockSpec((1, POOL_TILE, d), f)
    return pl.pallas_call(
        _pool_grid_kernel,
        grid=(b, nt),
        in_specs=[
            xspec(lambda i, t: (i, jnp.maximum(t - 1, 0), 0)),
            xspec(lambda i, t: (i, t, 0)),
            xspec(lambda i, t: (i, jnp.minimum(t + 1, nt - 1), 0)),
            pl.BlockSpec((1, 8, d), lambda i, t: (i, 0, 0)),
            pl.BlockSpec((1, d), lambda i, t: (0, 0)),
            pl.BlockSpec((ng, POOL_TILE, 3 * POOL_TILE), lambda i, t: (0, 0, 0)),
            pl.BlockSpec((POOL_TILE, 128), lambda i, t: (t, 0)),
            pl.BlockSpec((ng, POOL_GROUP, POOL_GROUP), lambda i, t: (0, 0, 0)),
            pl.BlockSpec((1, d), lambda i, t: (0, 0)),
        ],
        out_specs=pl.BlockSpec((1, POOL_TILE, d), lambda i, t: (i, t, 0)),
        out_shape=jax.ShapeDtypeStruct(hx.shape, F32),
        compiler_params=_cparams("arbitrary", "arbitrary"),
        name="pool_grid",
    )(hx, hx, hx, mod, gain, s, invc, pool_w, pool_scale)


def _pool_mixer_seq(hc, mod, gain, pool_w, pool_scale):
    b, l, d = hc.shape
    s_np, invc_np = _pool_constants_seq(l)
    s = jnp.asarray(s_np, BF16)
    invc = jnp.asarray(invc_np)
    ng = len(POOL_WINDOWS)
    return pl.pallas_call(
        _pool_seq_kernel,
        grid=(b,),
        in_specs=[
            pl.BlockSpec((1, l, d), lambda i: (i, 0, 0)),
            pl.BlockSpec((1, 8, d), lambda i: (0, 0, 0)),
            pl.BlockSpec((1, d), lambda i: (0, 0)),
            pl.BlockSpec((ng, l, l), lambda i: (0, 0, 0)),
            pl.BlockSpec((l, 128), lambda i: (0, 0)),
            pl.BlockSpec((ng, POOL_GROUP, POOL_GROUP), lambda i: (0, 0, 0)),
            pl.BlockSpec((1, d), lambda i: (0, 0)),
        ],
        out_specs=pl.BlockSpec((1, l, d), lambda i: (i, 0, 0)),
        out_shape=jax.ShapeDtypeStruct(hc.shape, F32),
        compiler_params=_cparams("arbitrary"),
        name="pool_seq",
    )(hc, mod, gain, s, invc, pool_w, pool_scale)


def _bitonic_merge(x):
    n = len(x)
    if n == 1:
        return x
    h = n // 2
    hi = [jnp.maximum(x[i], x[i + h]) for i in range(h)]
    lo = [jnp.minimum(x[i], x[i + h]) for i in range(h)]
    return _bitonic_merge(hi) + _bitonic_merge(lo)


def _sort_desc(x):
    n = len(x)
    if n == 1:
        return x
    return _bitonic_merge(_sort_desc(x[:n // 2]) + _sort_desc(x[n // 2:])[::-1])


def _top_merge(a, b):
    n = len(a)
    return _bitonic_merge([jnp.maximum(a[i], b[n - 1 - i]) for i in range(n)])


def _top16_rows(s):
    x = _sort_desc([s[i * 8:(i + 1) * 8] for i in range(PEER_TOPK)])
    for shift in (4, 2, 1):
        x = _top_merge(x, [pltpu.roll(v, shift, axis=0) for v in x])
    return x


def _top16_pairs(a, b, op, pad):
    c = lambda i, j: op(a[i - 1], b[j - 1])
    x = [c(1, j) for j in range(1, 17)]
    y = _bitonic_merge([c(2, j) for j in range(1, 9)] + [c(i, 1) for i in range(16, 8, -1)])
    z = _sort_desc([c(i, j) for i, nj in ((3, 5), (4, 4), (5, 3), (6, 2), (7, 2)) for j in range(1, nj + 1)])
    w = [c(8, 1), c(8, 2)] + [pad] * 14
    return _top_merge(_top_merge(_top_merge(x, y), z), w)


def _route_kernel(h_ref, mod_ref, g_ref, wqh_ref, wql_ref, keys_ref, hmt_ref, e1_ref, e2_ref, th_ref):
    mod = mod_ref[0]
    hm = _modulate(h_ref[...], g_ref[...], mod[3:4], mod[4:5])
    hmt = hm.T
    hi, lo = _split(hmt)
    hmt_ref[...] = hi
    wqh = wqh_ref[...]
    qt = _dot(wqh, hi) + _dot(wqh, lo) + _dot(wql_ref[...], hi)
    n = qt.shape[1]
    scores, tops = [], []
    for hp in range(2 * PEER_HEADS):
        s = _dot3(keys_ref[hp], qt[hp * PEER_N_KEYS:(hp + 1) * PEER_N_KEYS])
        scores.append(s)
        tops.append(_top16_rows(s))
    sub = lax.broadcasted_iota(jnp.int32, (8, n), 0)

    def stack_heads(p, i):
        out = tops[p][i]
        for h in range(1, PEER_HEADS):
            out = jnp.where(sub == h, tops[2 * h + p][i], out)
        return out

    a = [stack_heads(0, i) for i in range(PEER_TOPK)]
    b = [stack_heads(1, i) for i in range(PEER_TOPK)]
    csum = _top16_pairs(a, b, lambda u, v: u + v, jnp.full((8, n), -jnp.inf, F32))
    z = jnp.exp(csum[1] - csum[0]) + 1.0
    for kk in range(2, PEER_TOPK):
        z = z + jnp.exp(csum[kk] - csum[0])
    rz = 1.0 / z
    en = [jnp.exp(a[i] - a[0]) * rz for i in range(PEER_TOPK)]
    eb = [jnp.exp(b[i] - b[0]) for i in range(PEER_TOPK)]
    cprod = _top16_pairs(en, eb, lambda u, v: u * v, jnp.full((8, n), -1.0, F32))
    th_ref[...] = cprod[PEER_TOPK - 1]
    for h in range(PEER_HEADS):
        a1 = tops[2 * h][0][0:1]
        b1 = tops[2 * h + 1][0][0:1]
        e1_ref[h] = jnp.exp(scores[2 * h] - a1) * rz[h:h + 1]
        e2_ref[h] = jnp.exp(scores[2 * h + 1] - b1)


def _peer_route(h2d, mod, gain, wqt_hi, wqt_lo, keys, tiles_per_mod):
    t, d = h2d.shape
    n = ROUTE_TILE
    nq = wqt_hi.shape[0]
    modmap = (lambda i: (i // tiles_per_mod, 0, 0)) if tiles_per_mod else (lambda i: (0, 0, 0))
    return pl.pallas_call(
        _route_kernel,
        grid=(t // n,),
        in_specs=[
            pl.BlockSpec((n, d), lambda i: (i, 0)),
            pl.BlockSpec((1, 8, d), modmap),
            pl.BlockSpec((1, d), lambda i: (0, 0)),
            pl.BlockSpec((nq, d), lambda i: (0, 0)),
            pl.BlockSpec((nq, d), lambda i: (0, 0)),
            pl.BlockSpec((2 * PEER_HEADS, PEER_N_KEYS, PEER_N_KEYS), lambda i: (0, 0, 0)),
        ],
        out_specs=[
            pl.BlockSpec((d, n), lambda i: (0, i)),
            pl.BlockSpec((PEER_HEADS, PEER_N_KEYS, n), lambda i: (0, 0, i)),
            pl.BlockSpec((PEER_HEADS, PEER_N_KEYS, n), lambda i: (0, 0, i)),
            pl.BlockSpec((PEER_HEADS, n), lambda i: (0, i)),
        ],
        out_shape=[
            jax.ShapeDtypeStruct((d, t), BF16),
            jax.ShapeDtypeStruct((PEER_HEADS, PEER_N_KEYS, t), F32),
            jax.ShapeDtypeStruct((PEER_HEADS, PEER_N_KEYS, t), F32),
            jax.ShapeDtypeStruct((PEER_HEADS, t), F32),
        ],
        compiler_params=_cparams("arbitrary"),
        name="peer_route",
    )(h2d, mod, gain, wqt_hi, wqt_lo, keys)


def _peer_kernel(*refs, final):
    if final:
        h_ref, mod_ref, hmt_ref, e1_ref, e2_ref, th_ref, u_ref, vt_ref, fg_ref, o_ref, acc_ref, ga_ref = refs
    else:
        h_ref, mod_ref, hmt_ref, e1_ref, e2_ref, th_ref, u_ref, vt_ref, o_ref, acc_ref, ga_ref = refs
    j = pl.program_id(1)
    nchunk = PEER_EXPERTS // PEER_N_KEYS

    @pl.when(j == 0)
    def _():
        acc_ref[...] = jnp.zeros_like(acc_ref)

    act = _dot(u_ref[...], hmt_ref[...])
    for c in range(nchunk):
        i1 = j * nchunk + c
        g = None
        for h in range(PEER_HEADS):
            p = e1_ref[h, pl.ds(i1, 1), :] * e2_ref[h]
            sel = jnp.where(p >= th_ref[h:h + 1, :], p, 0.0)
            g = sel if g is None else g + sel
        a = act[c * PEER_N_KEYS:(c + 1) * PEER_N_KEYS]
        gelu = 0.5 * a * (1.0 + lax.erf(a * _SQRT_HALF))
        ga_ref[c * PEER_N_KEYS:(c + 1) * PEER_N_KEYS, :] = (g * gelu).astype(BF16)
    acc_ref[...] += _dot(vt_ref[...], ga_ref[...])

    @pl.when(j == pl.num_programs(1) - 1)
    def _():
        res = h_ref[...] + mod_ref[0][5:6] * acc_ref[...].T
        if final:
            ms = jnp.mean(res * res, axis=-1, keepdims=True)
            res = res * lax.rsqrt(ms + NORM_EPS) * fg_ref[...]
        o_ref[...] = res


def _peer_experts(h2d, mod, hmt, e1, e2, th, u_bf, vt_bf, tiles_per_mod, final_g=None):
    t, d = h2d.shape
    n, eb = PEER_TOKENS, PEER_EXPERTS
    modmap = (lambda i, j: (i // tiles_per_mod, 0, 0)) if tiles_per_mod else (lambda i, j: (0, 0, 0))
    in_specs = [
        pl.BlockSpec((n, d), lambda i, j: (i, 0)),
        pl.BlockSpec((1, 8, d), modmap),
        pl.BlockSpec((d, n), lambda i, j: (0, i)),
        pl.BlockSpec((PEER_HEADS, PEER_N_KEYS, n), lambda i, j: (0, 0, i)),
        pl.BlockSpec((PEER_HEADS, PEER_N_KEYS, n), lambda i, j: (0, 0, i)),
        pl.BlockSpec((PEER_HEADS, n), lambda i, j: (0, i)),
        pl.BlockSpec((eb, d), lambda i, j: (j, 0)),
        pl.BlockSpec((d, eb), lambda i, j: (0, j)),
    ]
    args = [h2d, mod, hmt, e1, e2, th, u_bf, vt_bf]
    if final_g is not None:
        in_specs.append(pl.BlockSpec((1, d), lambda i, j: (0, 0)))
        args.append(final_g)
    return pl.pallas_call(
        functools.partial(_peer_kernel, final=final_g is not None),
        grid=(t // n, PEER_N_EXPERTS // eb),
        in_specs=in_specs,
        out_specs=pl.BlockSpec((n, d), lambda i, j: (i, 0)),
        out_shape=jax.ShapeDtypeStruct((t, d), F32),
        scratch_shapes=[pltpu.VMEM((d, n), F32), pltpu.VMEM((eb, n), BF16)],
        compiler_params=_cparams("arbitrary", "arbitrary"),
        name="peer_experts",
    )(*args)


def _peer_layer(h, mod, gain, wq, keys, u_tab, v_tab, per_batch_mod, final_g=None):
    b, l, d = h.shape
    h2d = h.reshape(b * l, d)
    wqt = wq.T
    wqt_hi = wqt.astype(BF16)
    wqt_lo = (wqt - wqt_hi.astype(F32)).astype(BF16)
    keys2 = keys.reshape(2 * PEER_HEADS, PEER_N_KEYS, keys.shape[-1])
    hmt, e1, e2, th = _peer_route(h2d, mod, gain, wqt_hi, wqt_lo, keys2,
                                  l // ROUTE_TILE if per_batch_mod else 0)
    out = _peer_experts(h2d, mod, hmt, e1, e2, th, u_tab.astype(BF16), v_tab.T.astype(BF16),
                        l // PEER_TOKENS if per_batch_mod else 0, final_g)
    return out.reshape(b, l, d)


def _softplus(x):
    return jnp.maximum(x, 0.0) + jnp.log1p(jnp.exp(-jnp.abs(x)))


def _dnproj_kernel(h_ref, mod_ref, g_ref, wq_ref, wk_ref, wv_ref, wz_ref, cq_ref, ck_ref, cv_ref,
                   wab_ref, abp_ref, q_ref, k_ref, v_ref, z_ref, gb_ref, ax_ref):
    hb = pl.program_id(1)

    @pl.when(hb == 0)
    def _():
        mod = mod_ref[0]
        axb = _modulate(h_ref[0], g_ref[...], mod[0:1], mod[1:2]).astype(BF16)
        ax_ref[...] = axb
        ab = _dot(axb, wab_ref[...])
        prm = abp_ref[...]
        gcol = -jnp.exp(prm[0:1]) * _softplus(ab + prm[1:2])
        gb_ref[0] = jnp.where(prm[2:3] > 0.5, gcol, jax.nn.sigmoid(ab))

    axb = ax_ref[...]
    n = axb.shape[0]
    row = lax.broadcasted_iota(jnp.int32, (n, 2 * DN_HEAD_DIM), 0)

    def conv_silu(p, cw):
        y = p * cw[DN_CONV // 2:DN_CONV // 2 + 1]
        for kk in range(DN_CONV):
            dlt = kk - DN_CONV // 2
            if dlt == 0:
                continue
            shifted = pltpu.roll(p, (-dlt) % n, axis=0)
            valid = (row + dlt >= 0) & (row + dlt < n)
            y = y + jnp.where(valid, shifted, 0.0) * cw[kk:kk + 1]
        return _silu(y)

    def heads(u, ref, normalise, scale):
        for hh in range(2):
            uh = u[:, hh * DN_HEAD_DIM:(hh + 1) * DN_HEAD_DIM]
            if normalise:
                uh = uh * lax.rsqrt(jnp.sum(uh * uh, axis=-1, keepdims=True) + NORM_EPS)
            ref[0, hh] = uh * scale if scale != 1.0 else uh

    heads(conv_silu(_dot(axb, wq_ref[...]), cq_ref[...]), q_ref, True, DN_HEAD_DIM ** -0.5)
    heads(conv_silu(_dot(axb, wk_ref[...]), ck_ref[...]), k_ref, True, 1.0)
    heads(conv_silu(_dot(axb, wv_ref[...]), cv_ref[...]), v_ref, False, 1.0)
    z_ref[0] = _dot(axb, wz_ref[...])


def _dn_project(h, mod, gain, w4, conv_w, wab, abp, per_batch_mod):
    b, l, d = h.shape
    nhb = DN_HEADS // 2
    cw = 2 * DN_HEAD_DIM
    modmap = (lambda i, j: (i, 0, 0)) if per_batch_mod else (lambda i, j: (0, 0, 0))
    wspec = lambda off: pl.BlockSpec((d, cw), lambda i, j: (0, off * nhb + j))
    cspec = lambda off: pl.BlockSpec((DN_CONV, cw), lambda i, j: (0, off * nhb + j))
    hspec = pl.BlockSpec((1, 2, l, DN_HEAD_DIM), lambda i, j: (i, j, 0, 0))
    hshape = jax.ShapeDtypeStruct((b, DN_HEADS, l, DN_HEAD_DIM), F32)
    return pl.pallas_call(
        _dnproj_kernel,
        grid=(b, nhb),
        in_specs=[
            pl.BlockSpec((1, l, d), lambda i, j: (i, 0, 0)),
            pl.BlockSpec((1, 8, d), modmap),
            pl.BlockSpec((1, d), lambda i, j: (0, 0)),
            wspec(0), wspec(1), wspec(2), wspec(3),
            cspec(0), cspec(1), cspec(2),
            pl.BlockSpec((d, 128), lambda i, j: (0, 0)),
            pl.BlockSpec((8, 128), lambda i, j: (0, 0)),
        ],
        out_specs=[hspec, hspec, hspec,
                   pl.BlockSpec((1, l, cw), lambda i, j: (i, 0, j)),
                   pl.BlockSpec((1, l, 128), lambda i, j: (i, 0, 0))],
        out_shape=[hshape, hshape, hshape,
                   jax.ShapeDtypeStruct((b, l, d), F32),
                   jax.ShapeDtypeStruct((b, l, 128), F32)],
        scratch_shapes=[pltpu.VMEM((l, d), BF16)],
        compiler_params=_cparams("arbitrary", "arbitrary"),
        name="dn_project",
    )(h, mod, gain, w4, w4, w4, w4, conv_w, conv_w, conv_w, wab, abp)


def _dnprep_kernel(q_ref, k_ref, v_ref, gr_ref, gc_ref, u_ref, w_ref, kg_ref, qg_ref, at_ref, eg_ref,
                   x_ref, y_ref, t_ref, *, with_output):
    c = DN_CHUNK
    nseq, l = k_ref.shape[1], k_ref.shape[2]
    nch = l // c
    npb = nseq * 2 * nch
    ri = lax.broadcasted_iota(jnp.int32, (1, c, c), 1)
    ci = lax.broadcasted_iota(jnp.int32, (1, c, c), 2)
    lower = ri > ci
    chunked = lambda ref, s: ref[0, s].reshape(nch, c, ref.shape[-1])

    kept = {}
    for s in range(nseq):
        k3 = chunked(k_ref, s)
        k16 = k3.astype(BF16)
        kk = jnp.einsum('cid,cjd->cij', k16, k16, preferred_element_type=F32)
        grow = gr_ref[0, s]
        gcol = chunked(gc_ref, s)
        for dr in range(2):
            g_r, b_r = grow[:, 2 * dr:2 * dr + 1, :], grow[:, 2 * dr + 1:2 * dr + 2, :]
            g_c, b_c = gcol[:, :, 2 * dr:2 * dr + 1], gcol[:, :, 2 * dr + 1:2 * dr + 2]
            incl = (ri >= ci) if dr == 0 else (ri <= ci)
            incl_t = (ri <= ci) if dr == 0 else (ri >= ci)
            cum_c = jnp.sum(jnp.where(incl, g_r, 0.0), axis=2, keepdims=True)
            cum_r = jnp.sum(jnp.where(incl_t, g_c, 0.0), axis=1, keepdims=True)
            tot = jnp.sum(g_r, axis=2, keepdims=True)
            dec = jnp.exp(-jnp.abs(cum_c - cum_r))
            alow = jnp.where(lower, (b_c if dr == 0 else b_r) * kk * dec, 0.0)
            x_ref[pl.ds((s * 2 + dr) * nch * c, nch * c), :] = alow.reshape(nch * c, c)
            kept[s, dr] = (b_c, cum_c, tot, dec, incl)

    for i in range(c):
        y_ref[i] = x_ref[pl.ds(i, npb, stride=c), :].T

    nblk = c // 8
    for i in range(c):
        yi = y_ref[i]
        acc = [[-yi[b * 8:(b + 1) * 8] for b in range(nblk)], [None] * nblk]
        for j in range(1, i):
            nb_ = (j + 7) // 8
            tj = t_ref[j, 0:nb_ * 8, :]
            row = yi[j:j + 1, :]
            side = acc[j % 2]
            for b in range(nb_):
                term = row * tj[b * 8:(b + 1) * 8]
                side[b] = -term if side[b] is None else side[b] - term
        t_ref[i] = jnp.concatenate(
            [acc[0][b] if acc[1][b] is None else acc[0][b] + acc[1][b] for b in range(nblk)], axis=0)

    for i in range(c):
        x_ref[pl.ds(i, npb, stride=c), :] = t_ref[i].T

    for s in range(nseq):
        k3 = chunked(k_ref, s)
        v3 = chunked(v_ref, s)
        if with_output:
            q3 = chunked(q_ref, s)
            qk = jnp.einsum('cid,cjd->cij', q3.astype(BF16), k3.astype(BF16), preferred_element_type=F32)
        for dr in range(2):
            b_c, cum_c, tot, dec, incl = kept[s, dr]
            r16 = x_ref[pl.ds((s * 2 + dr) * nch * c, nch * c), :].reshape(nch, c, c).astype(BF16)
            apply = 'cij,cjd->cid' if dr == 0 else 'cji,cjd->cid'
            ecum = jnp.exp(cum_c)
            vb = v3 * b_c
            kbg = k3 * (b_c * ecum)
            u = vb + jnp.einsum(apply, r16, vb.astype(BF16), preferred_element_type=F32)
            w = kbg + jnp.einsum(apply, r16, kbg.astype(BF16), preferred_element_type=F32)
            u_ref[0, s, dr] = u.reshape(l, DN_HEAD_DIM)
            w_ref[0, s, dr] = w.reshape(l, DN_HEAD_DIM).astype(BF16)
            kg_ref[0, s, dr] = (k3 * jnp.exp(tot - cum_c)).reshape(l, DN_HEAD_DIM).astype(BF16)
            eg_ref[0, s, dr] = jnp.broadcast_to(jnp.exp(tot), (nch, 8, DN_HEAD_DIM))
            if with_output:
                qg_ref[0, s, dr] = (q3 * ecum).reshape(l, DN_HEAD_DIM).astype(BF16)
                at_ref[0, s, dr] = jnp.where(incl, qk * dec, 0.0).reshape(l, c).astype(BF16)
            else:
                qg_ref[0, s, dr] = jnp.zeros((l, DN_HEAD_DIM), BF16)
                at_ref[0, s, dr] = jnp.zeros((l, c), BF16)


def _dn_prepare(q, k, v, g_rows, g_cols, with_output):
    b, hh, l, dh = q.shape
    c = DN_CHUNK
    nch = l // c
    nseq = DN_PROBLEMS // (2 * nch)
    qspec = pl.BlockSpec((1, nseq, l, dh), lambda i, h: (i, h, 0, 0))
    ospec = lambda w: pl.BlockSpec((1, nseq, 2, l, w), lambda i, h: (i, h, 0, 0, 0))
    oshape = lambda w, dt: jax.ShapeDtypeStruct((b, hh, 2, l, w), dt)
    return pl.pallas_call(
        functools.partial(_dnprep_kernel, with_output=with_output),
        grid=(b, hh // nseq),
        in_specs=[qspec, qspec, qspec,
                  pl.BlockSpec((1, nseq, nch, 8, c), lambda i, h: (i, h, 0, 0, 0)),
                  pl.BlockSpec((1, nseq, l, 8), lambda i, h: (i, h, 0, 0))],
        out_specs=[ospec(dh), ospec(dh), ospec(dh), ospec(dh), ospec(c),
                   pl.BlockSpec((1, nseq, 2, nch, 8, dh), lambda i, h: (i, h, 0, 0, 0, 0))],
        out_shape=[oshape(dh, F32), oshape(dh, BF16), oshape(dh, BF16), oshape(dh, BF16),
                   oshape(c, BF16),
                   jax.ShapeDtypeStruct((b, hh, 2, nch, 8, dh), F32)],
        scratch_shapes=[pltpu.VMEM((DN_PROBLEMS * c, c), F32),
                        pltpu.VMEM((c, c, DN_PROBLEMS), F32),
                        pltpu.VMEM((c, c, DN_PROBLEMS), F32)],
        compiler_params=_cparams("arbitrary", "arbitrary"),
        name="dn_prepare",
    )(q, k, v, g_rows, g_cols)


def _dnscan_kernel(*refs, with_output, has_init):
    idx = 0
    dirs = []
    for _ in range(2):
        dirs.append(refs[idx:idx + 6])
        idx += 6
    if has_init:
        s0_ref = refs[idx]
        idx += 1
    if with_output:
        of_ref, ob_ref = refs[idx:idx + 2]
        idx += 2
    sout_ref = refs[idx]
    s_ref = refs[idx + 1]
    t = pl.program_id(2)
    nh = s_ref.shape[1]
    c = DN_CHUNK
    nchunks = dirs[0][0].shape[3] // c

    @pl.when(t == 0)
    def _():
        if has_init:
            for dr in range(2):
                s_ref[dr] = s0_ref[0, :, dr]
        else:
            s_ref[...] = jnp.zeros_like(s_ref)

    for step in range(nchunks):
        for dr in range(2):
            u_ref, w_ref, kg_ref, qg_ref, at_ref, eg_ref = dirs[dr]
            ch = step if dr == 0 else nchunks - 1 - step
            rows = slice(ch * c, (ch + 1) * c)
            for h in range(nh):
                s = s_ref[dr, h]
                s16 = s.astype(BF16)
                v_new = u_ref[0, h, 0, rows, :] - _dot(w_ref[0, h, 0, rows, :], s16)
                vn16 = v_new.astype(BF16)
                if with_output:
                    o = _dot(qg_ref[0, h, 0, rows, :], s16) + _dot(at_ref[0, h, 0, rows, :], vn16)
                    oref = of_ref if dr == 0 else ob_ref
                    oref[0, rows, h * DN_HEAD_DIM:(h + 1) * DN_HEAD_DIM] = o
                upd = lax.dot_general(kg_ref[0, h, 0, rows, :], vn16, (((0,), (0,)), ((), ())),
                                      preferred_element_type=F32)
                s_ref[dr, h] = s * eg_ref[0, h, 0, ch, 0:1, :] + upd

    @pl.when(t == pl.num_programs(2) - 1)
    def _():
        for dr in range(2):
            sout_ref[0, :, dr] = s_ref[dr]


def _dn_scan(prep, s0, with_output):
    u = prep[0]
    b, hh, _, l, dh = u.shape
    blk = min(DN_BLOCK, l)
    nb = l // blk
    cpb = blk // DN_CHUNK
    nh = DN_SCAN_HEADS
    in_specs, args = [], []

    def dir_map(dr, extra):
        def index_map(i, g, t):
            return (i, g, dr, t if dr == 0 else nb - 1 - t) + (0,) * extra
        return index_map

    for dr in range(2):
        for arr in prep[:5]:
            in_specs.append(pl.BlockSpec((1, nh, 1, blk, arr.shape[-1]), dir_map(dr, 1)))
            args.append(arr)
        in_specs.append(pl.BlockSpec((1, nh, 1, cpb, 8, dh), dir_map(dr, 2)))
        args.append(prep[5])
    sspec = pl.BlockSpec((1, nh, 2, dh, dh), lambda i, g, t: (i, g, 0, 0, 0))
    if s0 is not None:
        in_specs.append(sspec)
        args.append(s0)
    out_specs, out_shape = [], []
    if with_output:
        out_specs += [pl.BlockSpec((1, blk, nh * dh), lambda i, g, t: (i, t, g)),
                      pl.BlockSpec((1, blk, nh * dh), lambda i, g, t: (i, nb - 1 - t, g))]
        out_shape += [jax.ShapeDtypeStruct((b, l, hh * dh), F32)] * 2
    out_specs.append(sspec)
    out_shape.append(jax.ShapeDtypeStruct((b, hh, 2, dh, dh), F32))
    return pl.pallas_call(
        functools.partial(_dnscan_kernel, with_output=with_output, has_init=s0 is not None),
        grid=(b, hh // nh, nb),
        in_specs=in_specs,
        out_specs=out_specs,
        out_shape=out_shape,
        scratch_shapes=[pltpu.VMEM((2, nh, dh, dh), F32)],
        compiler_params=_cparams("arbitrary", "arbitrary", "arbitrary"),
        name="dn_scan",
    )(*args)


def _dnout_kernel(of_ref, ob_ref, z_ref, h_ref, mod_ref, ng_ref, wo_ref, o_ref):
    o = of_ref[0] + ob_ref[0]
    z = z_ref[0]
    parts = []
    for h in range(DN_HEADS):
        cols = slice(h * DN_HEAD_DIM, (h + 1) * DN_HEAD_DIM)
        oh = o[:, cols]
        oh = oh * lax.rsqrt(jnp.mean(oh * oh, axis=-1, keepdims=True) + NORM_EPS) * ng_ref[...]
        parts.append(oh * _silu(z[:, cols]))
    y = _dot(jnp.concatenate(parts, axis=-1).astype(BF16), wo_ref[...])
    o_ref[0] = h_ref[0] + mod_ref[0][2:3] * y


def _dn_output(of, ob, z, h, mod, norm_g, w_out_bf):
    b, l, d = h.shape
    n = OUT_TILE
    tok = pl.BlockSpec((1, n, d), lambda i, t: (i, t, 0))
    return pl.pallas_call(
        _dnout_kernel,
        grid=(b, l // n),
        in_specs=[tok, tok, tok, tok,
                  pl.BlockSpec((1, 8, d), lambda i, t: (i, 0, 0)),
                  pl.BlockSpec((1, DN_HEAD_DIM), lambda i, t: (0, 0)),
                  pl.BlockSpec((d, d), lambda i, t: (0, 0))],
        out_specs=tok,
        out_shape=jax.ShapeDtypeStruct(h.shape, F32),
        compiler_params=_cparams("arbitrary", "arbitrary"),
        name="dn_output",
    )(of, ob, z, h, mod, norm_g, w_out_bf)


def _dn_gate_layouts(gb):
    b, l, _ = gb.shape
    g4 = gb[:, :, :4 * DN_HEADS].reshape(b, l, 2, 2, DN_HEADS)
    g4 = jnp.transpose(g4, (0, 4, 1, 2, 3)).reshape(b, DN_HEADS, l, 4)
    cols = jnp.concatenate([g4, jnp.zeros_like(g4)], axis=-1)
    rows = jnp.transpose(cols.reshape(b, DN_HEADS, l // DN_CHUNK, DN_CHUNK, 8), (0, 1, 2, 4, 3))
    return rows, cols


def _delta_layer(hx, hc, modx, modc, gain, w_in, conv_w, a_log, dt_bias, norm_g, w_out):
    d = D_MODEL
    w4 = w_in[:, :4 * d].astype(BF16)
    nab = 4 * DN_HEADS
    wab = jnp.zeros((d, 128), F32).at[:, :nab].set(w_in[:, 4 * d:]).astype(BF16)
    is_g = jnp.zeros((2, 2, DN_HEADS), F32).at[:, 0].set(1.0).reshape(-1)
    place = lambda p: jnp.zeros((2, 2, DN_HEADS), F32).at[:, 0].set(p).reshape(-1)
    abp = jnp.zeros((8, 128), F32)
    abp = abp.at[0, :nab].set(place(a_log)).at[1, :nab].set(place(dt_bias)).at[2, :nab].set(is_g)

    qx, kx, vx, zx, gbx = _dn_project(hx, modx, gain, w4, conv_w, wab, abp, True)
    qc, kc, vc, _, gbc = _dn_project(hc, modc, gain, w4, conv_w, wab, abp, False)
    prep_c = _dn_prepare(qc, kc, vc, *_dn_gate_layouts(gbc), with_output=False)
    prep_x = _dn_prepare(qx, kx, vx, *_dn_gate_layouts(gbx), with_output=True)
    (s_ctx,) = _dn_scan(prep_c, None, with_output=False)
    of, ob, _ = _dn_scan(prep_x, s_ctx, with_output=True)
    return _dn_output(of, ob, zx, hx, modx, norm_g.reshape(1, -1), w_out.astype(BF16))


def kernel(x, c, ctx, c_ctx, mod_w, mod_b, norm1_g, norm2_g, pool_w, pool_scale, dn_w_in, dn_conv_w,
           dn_a_log, dn_dt_bias, dn_norm_g, dn_w_out, peer_wq, peer_keys, peer_u, peer_v, final_g):
    bsz = x.shape[0]
    d = D_MODEL
    cc = jnp.zeros((16, d), F32).at[:bsz].set(c).at[bsz].set(c_ctx)
    mods = _mod_vectors(cc, mod_w, mod_b).reshape(DEPTH, 16, 6, d)
    mods = jnp.concatenate([mods, jnp.zeros((DEPTH, 16, 2, d), F32)], axis=2)
    modx = [mods[i, :bsz] for i in range(DEPTH)]
    modc = [mods[i, bsz:bsz + 1] for i in range(DEPTH)]
    row = lambda v: v.reshape(1, -1)

    hx = _pool_mixer_grid(x, modx[0], row(norm1_g[0]), pool_w[0], row(pool_scale[0]))
    hc = _pool_mixer_seq(ctx, modc[0], row(norm1_g[0]), pool_w[0], row(pool_scale[0]))
    hx = _peer_layer(hx, modx[0], row(norm2_g[0]), peer_wq[0], peer_keys[0], peer_u[0], peer_v[0], True)
    hc = _peer_layer(hc, modc[0], row(norm2_g[0]), peer_wq[0], peer_keys[0], peer_u[0], peer_v[0], False)

    hx = _delta_layer(hx, hc, modx[1], modc[1], row(norm1_g[1]), dn_w_in[0], dn_conv_w[0], dn_a_log[0],
                      dn_dt_bias[0], dn_norm_g[0], dn_w_out[0])
    hx = _peer_layer(hx, modx[1], row(norm2_g[1]), peer_wq[1], peer_keys[1], peer_u[1], peer_v[1], True,
                     final_g=row(final_g))
    return hx
```
